```python
import jax, jax.numpy as jnp
from jax import lax
import numpy as np

D_MODEL = 1024
BATCH = 8
SEQ = 4096
DEPTH = 4

MIX_WIDTH = D_MODEL
POOL_WINDOWS = (2, 4, 8, 16)
N_POOL_GROUPS = 4
POOL_WIDTH = D_MODEL // 4
POOL_GROUP = POOL_WIDTH // N_POOL_GROUPS
CONV_WIDTH = D_MODEL // 4
CONV_HEADS = 4
CONV_K = 3
HEAD_DIM = 64
N_Q_HEADS = (D_MODEL // 2) // HEAD_DIM
N_KV_HEADS = 2
GQA_GROUP = N_Q_HEADS // N_KV_HEADS
Q_WIDTH = N_Q_HEADS * HEAD_DIM
KV_WIDTH = N_KV_HEADS * HEAD_DIM
WINDOW = 128
BLOCK = 128
IN_SIZES = (POOL_WIDTH, CONV_WIDTH, CONV_WIDTH, CONV_WIDTH, Q_WIDTH, KV_WIDTH, KV_WIDTH)
IN_WIDTH = POOL_WIDTH + 3 * CONV_WIDTH + Q_WIDTH + 2 * KV_WIDTH
D_FF = 2816
EPS = 1e-6
NEG_INF = -1e30

kernel_name = "hybrid_pool_conv_swa_sink_block"


def rms_norm(x, g):
    xf = x.astype(jnp.float32)
    y = xf * lax.rsqrt(jnp.mean(xf * xf, axis=-1, keepdims=True) + EPS)
    return (y * g.astype(jnp.float32)).astype(x.dtype)


def causal_dwconv(u, w):
    k = w.shape[0]
    s = u.shape[1]
    up = jnp.pad(u, ((0, 0), (k - 1, 0), (0, 0)))
    y = up[:, 0:s, :] * w[0]
    for j in range(1, k):
        y = y + up[:, j:j + s, :] * w[j]
    return y


def pool_mixer(u, w_grp, scale):
    bn, s, _ = u.shape
    uf = u.astype(jnp.float32)
    cs = jnp.pad(jnp.cumsum(uf, axis=1), ((0, 0), (1, 0), (0, 0)))
    cnt_base = jnp.arange(1, s + 1, dtype=jnp.int32)
    outs = []
    for g, w in enumerate(POOL_WINDOWS):
        c = cs[:, :, g * POOL_GROUP:(g + 1) * POOL_GROUP]
        lagged = jnp.pad(c, ((0, 0), (w - 1, 0), (0, 0)))[:, :s, :]
        cnt = jnp.minimum(cnt_base, w).astype(jnp.float32)
        mean = (c[:, 1:, :] - lagged) / cnt[None, :, None]
        outs.append(mean - uf[:, :, g * POOL_GROUP:(g + 1) * POOL_GROUP])
    d = jnp.stack(outs, axis=2)
    y = jnp.einsum('bsgc,gcd->bsgd', d, w_grp.astype(jnp.float32)).reshape(bn, s, POOL_WIDTH)
    return (y * scale.astype(jnp.float32)).astype(u.dtype)


def swa_sink_attention(q, k, v, sinks):
    bn, s, _ = q.shape
    nb = s // BLOCK
    qb = q.reshape(bn, nb, BLOCK, N_KV_HEADS, GQA_GROUP, HEAD_DIM)
    kb = k.reshape(bn, nb, BLOCK, N_KV_HEADS, HEAD_DIM)
    vb = v.reshape(bn, nb, BLOCK, N_KV_HEADS, HEAD_DIM)
    pad = ((0, 0), (1, 0), (0, 0), (0, 0), (0, 0))
    kk = jnp.concatenate([jnp.pad(kb, pad)[:, :nb], kb], axis=2)
    vv = jnp.concatenate([jnp.pad(vb, pad)[:, :nb], vb], axis=2)
    scores = jnp.einsum('bnqhgd,bnkhd->bnhgqk', qb, kk,
                        preferred_element_type=jnp.float32) * (HEAD_DIM ** -0.5)
    blk = jnp.arange(nb, dtype=jnp.int32)[:, None, None] * BLOCK
    qpos = blk + jnp.arange(BLOCK, dtype=jnp.int32)[None, :, None]
    kpos = blk - BLOCK + jnp.arange(2 * BLOCK, dtype=jnp.int32)[None, None, :]
    diff = qpos - kpos
    mask = (diff >= 0) & (diff < WINDOW) & (kpos >= 0)
    scores = jnp.where(mask[None, :, None, None, :, :], scores, NEG_INF)
    sink = jnp.broadcast_to(
        sinks.astype(jnp.float32).reshape(1, 1, N_KV_HEADS, GQA_GROUP, 1, 1),
        scores.shape[:-1] + (1,))
    probs = jax.nn.softmax(jnp.concatenate([scores, sink], axis=-1), axis=-1)[..., :2 * BLOCK]
    out = jnp.einsum('bnhgqk,bnkhd->bnqhgd', probs.astype(vv.dtype), vv)
    return out.reshape(bn, s, Q_WIDTH)


def setup_inputs(seed: int = 0) -> dict:
    key = jax.random.key(seed)
    ks = jax.random.split(key, 16)
    f32 = jnp.float32

    def nrm(k, shape, scale):
        return jax.random.normal(k, shape, f32) * scale

    def gain(k, width):
        return 1.0 + 0.1 * jax.random.normal(k, (DEPTH, width), f32)

    return {
        "x": nrm(ks[0], (BATCH, SEQ, D_MODEL), 1.0),
        "norm_mix_pre": gain(ks[1], D_MODEL),
        "w_in": nrm(ks[2], (DEPTH, D_MODEL, IN_WIDTH), D_MODEL ** -0.5),
        "pool_w": nrm(ks[3], (DEPTH, N_POOL_GROUPS, POOL_GROUP, POOL_GROUP), POOL_GROUP ** -0.5),
        "pool_scale": gain(ks[4], POOL_WIDTH),
        "conv_w": nrm(ks[5], (DEPTH, CONV_K, CONV_WIDTH), CONV_K ** -0.5),
        "attn_sinks": nrm(ks[6], (DEPTH, N_Q_HEADS), 0.5),
        "w_o": nrm(ks[7], (DEPTH, MIX_WIDTH, D_MODEL), MIX_WIDTH ** -0.5),
        "norm_mix_post": gain(ks[8], D_MODEL),
        "norm_ffn_pre": gain(ks[9], D_MODEL),
        "ffn_w_up": nrm(ks[10], (DEPTH, D_MODEL, 2 * D_FF), D_MODEL ** -0.5),
        "ffn_conv_w": nrm(ks[11], (DEPTH, CONV_K, 2 * D_FF), CONV_K ** -0.5),
        "ffn_conv_b": nrm(ks[12], (DEPTH, 2 * D_FF), 0.02),
        "ffn_w_down": nrm(ks[13], (DEPTH, D_FF, D_MODEL), D_FF ** -0.5),
        "norm_ffn_post": gain(ks[14], D_MODEL),
    }


def reference(x, norm_mix_pre, w_in, pool_w, pool_scale, conv_w, attn_sinks, w_o,
              norm_mix_post, norm_ffn_pre, ffn_w_up, ffn_conv_w, ffn_conv_b, ffn_w_down,
              norm_ffn_post):
    offsets = []
    acc = 0
    for sz in IN_SIZES[:-1]:
        acc += sz
        offsets.append(acc)
    for l in range(DEPTH):
        h = rms_norm(x, norm_mix_pre[l])
        proj = jnp.einsum('bsd,de->bse', h, w_in[l])
        u_pool, g_b, g_c, u_conv, q, k, v = jnp.split(proj, offsets, axis=-1)
        y_pool = pool_mixer(u_pool, pool_w[l], pool_scale[l])
        y_conv = g_b * causal_dwconv(g_c * u_conv, conv_w[l])
        y_attn = swa_sink_attention(q, k, v, attn_sinks[l])
        mix = jnp.einsum('bse,ed->bsd', jnp.concatenate([y_pool, y_conv, y_attn], axis=-1), w_o[l])
        x = x + rms_norm(mix, norm_mix_post[l])
        h = rms_norm(x, norm_ffn_pre[l])
        up = causal_dwconv(jnp.einsum('bsd,df->bsf', h, ffn_w_up[l]), ffn_conv_w[l]) + ffn_conv_b[l]
        gate, val = jnp.split(up, 2, axis=-1)
        ff = jnp.einsum('bsf,fd->bsd', jax.nn.silu(gate) * val, ffn_w_down[l])
        x = x + rms_norm(ff, norm_ffn_post[l])
    return x
```

```python
import functools

import jax
import jax.numpy as jnp
from jax import lax
from jax.experimental import pallas as pl
from jax.experimental.pallas import tpu as pltpu

D_MODEL = 1024
POOL_WINDOWS = (2, 4, 8, 16)
POOL_GROUP = 64
POOL_WIDTH = 256
CONV_WIDTH = 256
CONV_K = 3
HEAD_DIM = 64
N_Q_HEADS = 8
N_KV_HEADS = 2
Q_WIDTH = N_Q_HEADS * HEAD_DIM
KV_WIDTH = N_KV_HEADS * HEAD_DIM
WINDOW = 128
BLOCK = 128
IN_WIDTH = POOL_WIDTH + 3 * CONV_WIDTH + Q_WIDTH + 2 * KV_WIDTH
D_FF = 2816
EPS = 1e-6
NEG_INF = -1e30

SUBLANES = 8
LANES = 128
POOL_TAIL = 16
SEQ_TILE = 512
FF_CHUNK = 256
VMEM_LIMIT_BYTES = 56 * 1024 * 1024

F32 = jnp.float32
BF16 = jnp.bfloat16


def _rms_scale(v, gain):
    ms = jnp.mean(v * v, axis=-1, keepdims=True)
    return v * lax.rsqrt(ms + EPS) * gain


def _shift_rows(tail, cur, shift):
    ext = jnp.concatenate([tail, cur], axis=0)
    return pltpu.roll(ext, shift, 0)[tail.shape[0]:]


def _mixer_kernel(layer, x_ref, gpre_ref, win_ref, poolw_ref, pscale_ref, convw_ref, sinks_ref,
                  bias_ref, wo_ref, gpost_ref, o_ref,
                  kz_ref, vz_ref, pool_tail_ref, conv_tail_ref, mix_ref):
    ts = x_ref.shape[0]
    i = pl.program_id(1)

    @pl.when(i == 0)
    def _():
        kz_ref[:, 0:BLOCK, :] = jnp.zeros((4, BLOCK, LANES), BF16)
        vz_ref[:, 0:BLOCK, :] = jnp.zeros((4, BLOCK, LANES), BF16)
        pool_tail_ref[...] = jnp.zeros_like(pool_tail_ref)
        conv_tail_ref[...] = jnp.zeros_like(conv_tail_ref)

    x = x_ref[...]
    hb = _rms_scale(x, gpre_ref[...]).astype(BF16)

    pc = jnp.dot(hb, win_ref[:, 0:POOL_WIDTH + 3 * CONV_WIDTH], preferred_element_type=F32)
    u = pc[:, 0:POOL_WIDTH]
    ext = jnp.concatenate([pool_tail_ref[...], u], axis=0)
    a2 = ext + pltpu.roll(ext, 1, 0)
    a4 = a2 + pltpu.roll(a2, 2, 0)
    a8 = a4 + pltpu.roll(a4, 4, 0)
    a16 = a8 + pltpu.roll(a8, 8, 0)
    lane = lax.broadcasted_iota(jnp.int32, (ts, POOL_WIDTH), 1)
    grp = lane // POOL_GROUP
    wsum = jnp.where(grp == 0, a2[POOL_TAIL:],
                     jnp.where(grp == 1, a4[POOL_TAIL:],
                               jnp.where(grp == 2, a8[POOL_TAIL:], a16[POOL_TAIL:])))
    win = jnp.where(grp == 0, POOL_WINDOWS[0],
                    jnp.where(grp == 1, POOL_WINDOWS[1],
                              jnp.where(grp == 2, POOL_WINDOWS[2], POOL_WINDOWS[3])))
    pos = i * ts + lax.broadcasted_iota(jnp.int32, (ts, POOL_WIDTH), 0)
    cnt = jnp.minimum(pos + 1, win).astype(F32)
    d = wsum / cnt - u
    y_pool = jnp.dot(d.astype(BF16), poolw_ref[...], preferred_element_type=F32) * pscale_ref[...]
    mix_ref[:, 0:POOL_WIDTH] = y_pool.astype(BF16)
    pool_tail_ref[...] = u[ts - POOL_TAIL:]

    g_b = pc[:, POOL_WIDTH:POOL_WIDTH + CONV_WIDTH]
    g_c = pc[:, POOL_WIDTH + CONV_WIDTH:POOL_WIDTH + 2 * CONV_WIDTH]
    u_c = pc[:, POOL_WIDTH + 2 * CONV_WIDTH:POOL_WIDTH + 3 * CONV_WIDTH]
    c = g_c * u_c
    tail = conv_tail_ref[...]
    cw = convw_ref[...]
    y_conv = g_b * (cw[0:1] * _shift_rows(tail, c, 2) + cw[1:2] * _shift_rows(tail, c, 1)
                    + cw[2:3] * c)
    mix_ref[:, POOL_WIDTH:POOL_WIDTH + CONV_WIDTH] = y_conv.astype(BF16)
    conv_tail_ref[...] = c[ts - SUBLANES:]

    q_off = POOL_WIDTH + 3 * CONV_WIDTH
    q = jnp.dot(hb, win_ref[:, q_off:q_off + Q_WIDTH], preferred_element_type=F32)
    qb = (q * (HEAD_DIM ** -0.5)).astype(BF16)
    kv = jnp.dot(hb, win_ref[:, q_off + Q_WIDTH:IN_WIDTH], preferred_element_type=F32)
    lo = lax.broadcasted_iota(jnp.int32, (ts, LANES), 1) < HEAD_DIM
    for src, dst in ((kv[:, 0:KV_WIDTH], kz_ref), (kv[:, KV_WIDTH:2 * KV_WIDTH], vz_ref)):
        swapped = pltpu.roll(src, HEAD_DIM, 1)
        zero = jnp.zeros_like(src)
        dst[0, BLOCK:BLOCK + ts, :] = jnp.where(lo, src, zero).astype(BF16)
        dst[1, BLOCK:BLOCK + ts, :] = jnp.where(lo, zero, swapped).astype(BF16)
        dst[2, BLOCK:BLOCK + ts, :] = jnp.where(lo, swapped, zero).astype(BF16)
        dst[3, BLOCK:BLOCK + ts, :] = jnp.where(lo, zero, src).astype(BF16)

    row = lax.broadcasted_iota(jnp.int32, (2 * BLOCK, 1), 0)
    first = jnp.where(i == 0, 1, 0)
    for n in range(ts // BLOCK):
        bias = bias_ref[first] if n == 0 else bias_ref[0]
        qrows = slice(n * BLOCK, (n + 1) * BLOCK)
        krows = slice(n * BLOCK, n * BLOCK + 2 * BLOCK)
        for hk in range(N_KV_HEADS):
            qa = jnp.concatenate([qb[qrows, (2 * hk) * LANES:(2 * hk + 1) * LANES],
                                  qb[qrows, (2 * hk + 1) * LANES:(2 * hk + 2) * LANES]], axis=0)
            acc = None
            for half in range(2):
                s = lax.dot_general(qa, kz_ref[2 * hk + half, krows, :],
                                    (((1,), (1,)), ((), ())), preferred_element_type=F32)
                s = s + bias
                head0 = 4 * hk + half
                sink = jnp.where(row < BLOCK, sinks_ref[layer, head0], sinks_ref[layer, head0 + 2])
                m = jnp.maximum(jnp.max(s, axis=-1, keepdims=True), sink)
                p = jnp.exp(s - m)
                denom = jnp.sum(p, axis=-1, keepdims=True) + jnp.exp(sink - m)
                o = jnp.dot(p.astype(BF16), vz_ref[2 * hk + half, krows, :],
                            preferred_element_type=F32)
                o = o / denom
                acc = o if acc is None else acc + o
            a_off = POOL_WIDTH + CONV_WIDTH + 2 * hk * LANES
            mix_ref[qrows, a_off:a_off + LANES] = acc[0:BLOCK].astype(BF16)
            mix_ref[qrows, a_off + LANES:a_off + 2 * LANES] = acc[BLOCK:2 * BLOCK].astype(BF16)

    kz_ref[:, 0:BLOCK, :] = kz_ref[:, ts:ts + BLOCK, :]
    vz_ref[:, 0:BLOCK, :] = vz_ref[:, ts:ts + BLOCK, :]

    mixed = jnp.dot(mix_ref[...], wo_ref[...], preferred_element_type=F32)
    o_ref[...] = x + _rms_scale(mixed, gpost_ref[...])


def _ffn_kernel(x_ref, gpre_ref, wup_ref, cw_ref, cb_ref, wdn_ref, gpost_ref, o_ref,
                tail_ref, act_ref):
    ts = x_ref.shape[0]
    i = pl.program_id(1)

    @pl.when(i == 0)
    def _():
        tail_ref[...] = jnp.zeros_like(tail_ref)

    x = x_ref[...]
    hb = _rms_scale(x, gpre_ref[...]).astype(BF16)

    def conv_cols(cols):
        up = jnp.dot(hb, wup_ref[:, cols], preferred_element_type=F32)
        tail = tail_ref[:, cols]
        cw = cw_ref[:, cols]
        out = (cw[0:1] * _shift_rows(tail, up, 2) + cw[1:2] * _shift_rows(tail, up, 1)
               + cw[2:3] * up + cb_ref[:, cols])
        tail_ref[:, cols] = up[ts - SUBLANES:]
        return out

    for c in range(D_FF // FF_CHUNK):
        gate = conv_cols(slice(c * FF_CHUNK, (c + 1) * FF_CHUNK))
        val = conv_cols(slice(D_FF + c * FF_CHUNK, D_FF + (c + 1) * FF_CHUNK))
        act = gate / (1.0 + jnp.exp(-gate)) * val
        act_ref[:, c * FF_CHUNK:(c + 1) * FF_CHUNK] = act.astype(BF16)

    ff = jnp.dot(act_ref[...], wdn_ref[...], preferred_element_type=F32)
    o_ref[...] = x + _rms_scale(ff, gpost_ref[...])


def _attention_bias():
    qi = jnp.arange(BLOCK, dtype=jnp.int32)[:, None]
    kj = jnp.arange(2 * BLOCK, dtype=jnp.int32)[None, :]
    diff = qi + BLOCK - kj
    band = (diff >= 0) & (diff < WINDOW)
    general = jnp.where(band, 0.0, NEG_INF).astype(F32)
    first = jnp.where(band & (kj >= BLOCK), 0.0, NEG_INF).astype(F32)
    return jnp.stack([jnp.tile(general, (2, 1)), jnp.tile(first, (2, 1))])


def _layer_spec(shape):
    return lambda layer: pl.BlockSpec((None,) + shape, lambda b, i: (layer,) + (0,) * len(shape))


def kernel(x, norm_mix_pre, w_in, pool_w, pool_scale, conv_w, attn_sinks, w_o, norm_mix_post,
           norm_ffn_pre, ffn_w_up, ffn_conv_w, ffn_conv_b, ffn_w_down, norm_ffn_post):
    batch, seq, d = x.shape
    depth = w_in.shape[0]
    assert d == D_MODEL and seq % SEQ_TILE == 0 and SEQ_TILE % BLOCK == 0
    ts = SEQ_TILE
    grid = (batch, seq // ts)

    w_in_b = w_in.astype(BF16)
    w_o_b = w_o.astype(BF16)
    w_up_b = ffn_w_up.astype(BF16)
    w_dn_b = ffn_w_down.astype(BF16)
    eye = jnp.eye(len(POOL_WINDOWS), dtype=F32)
    pool_bd = jnp.einsum('lgcd,gh->lgchd', pool_w, eye).reshape(depth, POOL_WIDTH, POOL_WIDTH)
    pool_bd = pool_bd.astype(BF16)
    bias = _attention_bias()

    def row3(a):
        return a.reshape(depth, 1, a.shape[-1])

    x_spec = pl.BlockSpec((None, ts, d), lambda b, i: (b, i, 0))
    params = pltpu.CompilerParams(dimension_semantics=("arbitrary", "arbitrary"),
                                  vmem_limit_bytes=VMEM_LIMIT_BYTES)

    for layer in range(depth):
        x = pl.pallas_call(
            functools.partial(_mixer_kernel, layer),
            grid=grid,
            in_specs=[
                x_spec,
                _layer_spec((1, d))(layer),
                _layer_spec((d, IN_WIDTH))(layer),
                _layer_spec((POOL_WIDTH, POOL_WIDTH))(layer),
                _layer_spec((1, POOL_WIDTH))(layer),
                _layer_spec((CONV_K, CONV_WIDTH))(layer),
                pl.BlockSpec(memory_space=pltpu.SMEM),
                pl.BlockSpec((2, 2 * BLOCK, 2 * BLOCK), lambda b, i: (0, 0, 0)),
                _layer_spec((d, d))(layer),
                _layer_spec((1, d))(layer),
            ],
            out_specs=x_spec,
            out_shape=jax.ShapeDtypeStruct(x.shape, x.dtype),
            scratch_shapes=[
                pltpu.VMEM((4, BLOCK + ts, LANES), BF16),
                pltpu.VMEM((4, BLOCK + ts, LANES), BF16),
                pltpu.VMEM((POOL_TAIL, POOL_WIDTH), F32),
                pltpu.VMEM((SUBLANES, CONV_WIDTH), F32),
                pltpu.VMEM((ts, d), BF16),
            ],
            compiler_params=params,
            name=f"mixer_l{layer}",
        )(x, row3(norm_mix_pre), w_in_b, pool_bd, row3(pool_scale), conv_w, attn_sinks, bias,
          w_o_b, row3(norm_mix_post))

        x = pl.pallas_call(
            _ffn_kernel,
            grid=grid,
            in_specs=[
                x_spec,
                _layer_spec((1, d))(layer),
                _layer_spec((d, 2 * D_FF))(layer),
                _layer_spec((CONV_K, 2 * D_FF))(layer),
                _layer_spec((1, 2 * D_FF))(layer),
                _layer_spec((D_FF, d))(layer),
                _layer_spec((1, d))(layer),
            ],
            out_specs=x_spec,
            out_shape=jax.ShapeDtypeStruct(x.shape, x.dtype),
            scratch_shapes=[
                pltpu.VMEM((SUBLANES, 2 * D_FF), F32),
                pltpu.VMEM((ts, D_FF), BF16),
            ],
            compiler_params=params,
            name=f"ffn_l{layer}",
        )(x, row3(norm_ffn_pre), w_up_b, ffn_conv_w, row3(ffn_conv_b), w_dn_b,
          row3(norm_ffn_post))
    return x
```

```python
import functools

import jax
import jax.numpy as jnp
from jax import lax
from jax.experimental import pallas as pl
from jax.experimental.pallas import tpu as pltpu

D_MODEL = 1024
POOL_WINDOWS = (2, 4, 8, 16)
POOL_GROUP = 64
POOL_WIDTH = 256
CONV_WIDTH = 256
CONV_K = 3
HEAD_DIM = 64
N_Q_HEADS = 8
N_KV_HEADS = 2
Q_WIDTH = N_Q_HEADS * HEAD_DIM
KV_WIDTH = N_KV_HEADS * HEAD_DIM
WINDOW = 128
BLOCK = 128
IN_WIDTH = POOL_WIDTH + 3 * CONV_WIDTH + Q_WIDTH + 2 * KV_WIDTH
D_FF = 2816
EPS = 1e-6
NEG_INF = -1e30

SUBLANES = 8
LANES = 128
POOL_TAIL = 16
SEQ_TILE = 512
FF_CHUNK = 256
VMEM_LIMIT_BYTES = 56 * 1024 * 1024

F32 = jnp.float32
BF16 = jnp.bfloat16


def _rms_scale(v, gain):
    ms = jnp.mean(v * v, axis=-1, keepdims=True)
    return v * lax.rsqrt(ms + EPS) * gain


def _causal_conv3(cur, tail, cw, buf_ref, slab0):
    ts, width = cur.shape
    delayed = []
    for j in range(width // LANES):
        sl = slice(j * LANES, (j + 1) * LANES)
        buf = buf_ref.at[slab0 + j]
        buf[pl.ds(0, SUBLANES, stride=2), :] = tail[:, sl]
        buf[pl.ds(2 * SUBLANES, ts, stride=2), :] = cur[:, sl]
        d1 = buf[pl.ds(2 * (SUBLANES - 1), ts, stride=2), :]
        d2 = buf[pl.ds(2 * (SUBLANES - 2), ts, stride=2), :]
        delayed.append(cw[0:1, sl] * d2 + cw[1:2, sl] * d1)
    return jnp.concatenate(delayed, axis=1) + cw[2:3] * cur


def _mixer_kernel(layer, x_ref, gpre_ref, win_ref, poolw_ref, pscale_ref, convw_ref, sinks_ref,
                  bias_ref, wo_ref, gpost_ref, o_ref,
                  kz_ref, vz_ref, pool_tail_ref, conv_tail_ref, mix_ref, shift_ref):
    ts = x_ref.shape[0]
    i = pl.program_id(1)

    @pl.when(i == 0)
    def _():
        kz_ref[:, 0:BLOCK, :] = jnp.zeros((4, BLOCK, LANES), BF16)
        vz_ref[:, 0:BLOCK, :] = jnp.zeros((4, BLOCK, LANES), BF16)
        pool_tail_ref[...] = jnp.zeros_like(pool_tail_ref)
        conv_tail_ref[...] = jnp.zeros_like(conv_tail_ref)

    x = x_ref[...]
    hb = _rms_scale(x, gpre_ref[...]).astype(BF16)

    pc = jnp.dot(hb, win_ref[:, 0:POOL_WIDTH + 3 * CONV_WIDTH], preferred_element_type=F32)
    u = pc[:, 0:POOL_WIDTH]
    ext = jnp.concatenate([pool_tail_ref[...], u], axis=0)
    a2 = ext + pltpu.roll(ext, 1, 0)
    a4 = a2 + pltpu.roll(a2, 2, 0)
    a8 = a4 + pltpu.roll(a4, 4, 0)
    a16 = a8 + pltpu.roll(a8, 8, 0)
    lane = lax.broadcasted_iota(jnp.int32, (ts, POOL_WIDTH), 1)
    grp = lane // POOL_GROUP
    wsum = jnp.where(grp == 0, a2[POOL_TAIL:],
                     jnp.where(grp == 1, a4[POOL_TAIL:],
                               jnp.where(grp == 2, a8[POOL_TAIL:], a16[POOL_TAIL:])))
    win = jnp.where(grp == 0, POOL_WINDOWS[0],
                    jnp.where(grp == 1, POOL_WINDOWS[1],
                              jnp.where(grp == 2, POOL_WINDOWS[2], POOL_WINDOWS[3])))
    pos = i * ts + lax.broadcasted_iota(jnp.int32, (ts, POOL_WIDTH), 0)
    cnt = jnp.minimum(pos + 1, win).astype(F32)
    d = wsum / cnt - u
    y_pool = jnp.dot(d.astype(BF16), poolw_ref[...], preferred_element_type=F32) * pscale_ref[...]
    mix_ref[:, 0:POOL_WIDTH] = y_pool.astype(BF16)
    pool_tail_ref[...] = u[ts - POOL_TAIL:]

    g_b = pc[:, POOL_WIDTH:POOL_WIDTH + CONV_WIDTH]
    g_c = pc[:, POOL_WIDTH + CONV_WIDTH:POOL_WIDTH + 2 * CONV_WIDTH]
    u_c = pc[:, POOL_WIDTH + 2 * CONV_WIDTH:POOL_WIDTH + 3 * CONV_WIDTH]
    c = g_c * u_c
    y_conv = g_b * _causal_conv3(c, conv_tail_ref[...], convw_ref[...], shift_ref, 0)
    mix_ref[:, POOL_WIDTH:POOL_WIDTH + CONV_WIDTH] = y_conv.astype(BF16)
    conv_tail_ref[...] = c[ts - SUBLANES:]

    q_off = POOL_WIDTH + 3 * CONV_WIDTH
    q = jnp.dot(hb, win_ref[:, q_off:q_off + Q_WIDTH], preferred_element_type=F32)
    qb = (q * (HEAD_DIM ** -0.5)).astype(BF16)
    kv = jnp.dot(hb, win_ref[:, q_off + Q_WIDTH:IN_WIDTH], preferred_element_type=F32)
    lo = lax.broadcasted_iota(jnp.int32, (ts, LANES), 1) < HEAD_DIM
    for src, dst in ((kv[:, 0:KV_WIDTH], kz_ref), (kv[:, KV_WIDTH:2 * KV_WIDTH], vz_ref)):
        swapped = pltpu.roll(src, HEAD_DIM, 1)
        zero = jnp.zeros_like(src)
        dst[0, BLOCK:BLOCK + ts, :] = jnp.where(lo, src, zero).astype(BF16)
        dst[1, BLOCK:BLOCK + ts, :] = jnp.where(lo, zero, swapped).astype(BF16)
        dst[2, BLOCK:BLOCK + ts, :] = jnp.where(lo, swapped, zero).astype(BF16)
        dst[3, BLOCK:BLOCK + ts, :] = jnp.where(lo, zero, src).astype(BF16)

    row = lax.broadcasted_iota(jnp.int32, (2 * BLOCK, 1), 0)
    first = jnp.where(i == 0, 1, 0)
    for n in range(ts // BLOCK):
        bias = bias_ref[first] if n == 0 else bias_ref[0]
        qrows = slice(n * BLOCK, (n + 1) * BLOCK)
        krows = slice(n * BLOCK, n * BLOCK + 2 * BLOCK)
        for hk in range(N_KV_HEADS):
            qa = jnp.concatenate([qb[qrows, (2 * hk) * LANES:(2 * hk + 1) * LANES],
                                  qb[qrows, (2 * hk + 1) * LANES:(2 * hk + 2) * LANES]], axis=0)
            acc = None
            for half in range(2):
                s = lax.dot_general(qa, kz_ref[2 * hk + half, krows, :],
                                    (((1,), (1,)), ((), ())), preferred_element_type=F32)
                s = s + bias
                head0 = 4 * hk + half
                sink = jnp.where(row < BLOCK, sinks_ref[layer, head0], sinks_ref[layer, head0 + 2])
                m = jnp.maximum(jnp.max(s, axis=-1, keepdims=True), sink)
                p = jnp.exp(s - m)
                denom = jnp.sum(p, axis=-1, keepdims=True) + jnp.exp(sink - m)
                o = jnp.dot(p.astype(BF16), vz_ref[2 * hk + half, krows, :],
                            preferred_element_type=F32)
                o = o / denom
                acc = o if acc is None else acc + o
            a_off = POOL_WIDTH + CONV_WIDTH + 2 * hk * LANES
            mix_ref[qrows, a_off:a_off + LANES] = acc[0:BLOCK].astype(BF16)
            mix_ref[qrows, a_off + LANES:a_off + 2 * LANES] = acc[BLOCK:2 * BLOCK].astype(BF16)

    kz_ref[:, 0:BLOCK, :] = kz_ref[:, ts:ts + BLOCK, :]
    vz_ref[:, 0:BLOCK, :] = vz_ref[:, ts:ts + BLOCK, :]

    mixed = jnp.dot(mix_ref[...], wo_ref[...], preferred_element_type=F32)
    o_ref[...] = x + _rms_scale(mixed, gpost_ref[...])


def _ffn_kernel(x_ref, gpre_ref, wup_ref, cw_ref, cb_ref, wdn_ref, gpost_ref, o_ref,
                tail_ref, act_ref, shift_ref):
    ts = x_ref.shape[0]
    i = pl.program_id(1)

    @pl.when(i == 0)
    def _():
        tail_ref[...] = jnp.zeros_like(tail_ref)

    x = x_ref[...]
    hb = _rms_scale(x, gpre_ref[...]).astype(BF16)

    def conv_cols(cols, slab0):
        up = jnp.dot(hb, wup_ref[:, cols], preferred_element_type=F32)
        out = _causal_conv3(up, tail_ref[:, cols], cw_ref[:, cols], shift_ref, slab0)
        tail_ref[:, cols] = up[ts - SUBLANES:]
        return out + cb_ref[:, cols]

    for c in range(D_FF // FF_CHUNK):
        gate = conv_cols(slice(c * FF_CHUNK, (c + 1) * FF_CHUNK), 0)
        val = conv_cols(slice(D_FF + c * FF_CHUNK, D_FF + (c + 1) * FF_CHUNK),
                        FF_CHUNK // LANES)
        act = gate / (1.0 + jnp.exp(-gate)) * val
        act_ref[:, c * FF_CHUNK:(c + 1) * FF_CHUNK] = act.astype(BF16)

    ff = jnp.dot(act_ref[...], wdn_ref[...], preferred_element_type=F32)
    o_ref[...] = x + _rms_scale(ff, gpost_ref[...])


def _attention_bias():
    qi = jnp.arange(BLOCK, dtype=jnp.int32)[:, None]
    kj = jnp.arange(2 * BLOCK, dtype=jnp.int32)[None, :]
    diff = qi + BLOCK - kj
    band = (diff >= 0) & (diff < WINDOW)
    general = jnp.where(band, 0.0, NEG_INF).astype(F32)
    first = jnp.where(band & (kj >= BLOCK), 0.0, NEG_INF).astype(F32)
    return jnp.stack([jnp.tile(general, (2, 1)), jnp.tile(first, (2, 1))])


def _layer_spec(shape):
    return lambda layer: pl.BlockSpec((None,) + shape, lambda b, i: (layer,) + (0,) * len(shape))


def kernel(x, norm_mix_pre, w_in, pool_w, pool_scale, conv_w, attn_sinks, w_o, norm_mix_post,
           norm_ffn_pre, ffn_w_up, ffn_conv_w, ffn_conv_b, ffn_w_down, norm_ffn_post):
    batch, seq, d = x.shape
    depth = w_in.shape[0]
    assert d == D_MODEL and seq % SEQ_TILE == 0 and SEQ_TILE % BLOCK == 0
    ts = SEQ_TILE
    grid = (batch, seq // ts)

    w_in_b = w_in.astype(BF16)
    w_o_b = w_o.astype(BF16)
    w_up_b = ffn_w_up.astype(BF16)
    w_dn_b = ffn_w_down.astype(BF16)
    eye = jnp.eye(len(POOL_WINDOWS), dtype=F32)
    pool_bd = jnp.einsum('lgcd,gh->lgchd', pool_w, eye).reshape(depth, POOL_WIDTH, POOL_WIDTH)
    pool_bd = pool_bd.astype(BF16)
    bias = _attention_bias()

    def row3(a):
        return a.reshape(depth, 1, a.shape[-1])

    x_spec = pl.BlockSpec((None, ts, d), lambda b, i: (b, i, 0))
    params = pltpu.CompilerParams(dimension_semantics=("arbitrary", "arbitrary"),
                                  vmem_limit_bytes=VMEM_LIMIT_BYTES)

    for layer in range(depth):
        x = pl.pallas_call(
            functools.partial(_mixer_kernel, layer),
            grid=grid,
            in_specs=[
                x_spec,
                _layer_spec((1, d))(layer),
                _layer_spec((d, IN_WIDTH))(layer),
                _layer_spec((POOL_WIDTH, POOL_WIDTH))(layer),
                _layer_spec((1, POOL_WIDTH))(layer),
                _layer_spec((CONV_K, CONV_WIDTH))(layer),
                pl.BlockSpec(memory_space=pltpu.SMEM),
                pl.BlockSpec((2, 2 * BLOCK, 2 * BLOCK), lambda b, i: (0, 0, 0)),
                _layer_spec((d, d))(layer),
                _layer_spec((1, d))(layer),
            ],
            out_specs=x_spec,
            out_shape=jax.ShapeDtypeStruct(x.shape, x.dtype),
            scratch_shapes=[
                pltpu.VMEM((4, BLOCK + ts, LANES), BF16),
                pltpu.VMEM((4, BLOCK + ts, LANES), BF16),
                pltpu.VMEM((POOL_TAIL, POOL_WIDTH), F32),
                pltpu.VMEM((SUBLANES, CONV_WIDTH), F32),
                pltpu.VMEM((ts, d), BF16),
                pltpu.VMEM((CONV_WIDTH // LANES, 2 * (SUBLANES + ts), LANES), F32),
            ],
            compiler_params=params,
            name=f"mixer_l{layer}",
        )(x, row3(norm_mix_pre), w_in_b, pool_bd, row3(pool_scale), conv_w, attn_sinks, bias,
          w_o_b, row3(norm_mix_post))

        x = pl.pallas_call(
            _ffn_kernel,
            grid=grid,
            in_specs=[
                x_spec,
                _layer_spec((1, d))(layer),
                _layer_spec((d, 2 * D_FF))(layer),
                _layer_spec((CONV_K, 2 * D_FF))(layer),
                _layer_spec((1, 2 * D_FF))(layer),
                _layer_spec((D_FF, d))(layer),
                _layer_spec((1, d))(layer),
            ],
            out_specs=x_spec,
            out_shape=jax.ShapeDtypeStruct(x.shape, x.dtype),
            scratch_shapes=[
                pltpu.VMEM((SUBLANES, 2 * D_FF), F32),
                pltpu.VMEM((ts, D_FF), BF16),
                pltpu.VMEM((2 * FF_CHUNK // LANES, 2 * (SUBLANES + ts), LANES), F32),
            ],
            compiler_params=params,
            name=f"ffn_l{layer}",
        )(x, row3(norm_ffn_pre), w_up_b, ffn_conv_w, row3(ffn_conv_b), w_dn_b,
          row3(norm_ffn_post))
    return x
```

```python
import functools

import jax
import jax.numpy as jnp
from jax import lax
from jax.experimental import pallas as pl
from jax.experimental.pallas import tpu as pltpu

D_MODEL = 1024
POOL_WINDOWS = (2, 4, 8, 16)
POOL_GROUP = 64
POOL_WIDTH = 256
CONV_WIDTH = 256
CONV_K = 3
HEAD_DIM = 64
N_Q_HEADS = 8
N_KV_HEADS = 2
Q_WIDTH = N_Q_HEADS * HEAD_DIM
KV_WIDTH = N_KV_HEADS * HEAD_DIM
WINDOW = 128
BLOCK = 128
IN_WIDTH = POOL_WIDTH + 3 * CONV_WIDTH + Q_WIDTH + 2 * KV_WIDTH
D_FF = 2816
EPS = 1e-6
NEG_INF = -1e30

SUBLANES = 8
LANES = 128
POOL_TAIL = 16
SEQ_TILE = 1024
FF_CHUNK = 256
VMEM_LIMIT_BYTES = 56 * 1024 * 1024

F32 = jnp.float32
BF16 = jnp.bfloat16


def _rms_scale(v, gain):
    ms = jnp.mean(v * v, axis=-1, keepdims=True)
    return v * lax.rsqrt(ms + EPS) * gain


def _causal_conv3(cur, tail, cw, buf_ref, slab0):
    ts, width = cur.shape
    delayed = []
    for j in range(width // LANES):
        sl = slice(j * LANES, (j + 1) * LANES)
        buf = buf_ref.at[slab0 + j]
        buf[pl.ds(0, SUBLANES, stride=2), :] = tail[:, sl]
        buf[pl.ds(2 * SUBLANES, ts, stride=2), :] = cur[:, sl]
        d1 = buf[pl.ds(2 * (SUBLANES - 1), ts, stride=2), :]
        d2 = buf[pl.ds(2 * (SUBLANES - 2), ts, stride=2), :]
        delayed.append(cw[0:1, sl] * d2 + cw[1:2, sl] * d1)
    return jnp.concatenate(delayed, axis=1) + cw[2:3] * cur


def _mixer_kernel(layer, x_ref, gpre_ref, win_ref, poolw_ref, pscale_ref, convw_ref, sinks_ref,
                  bias_ref, wo_ref, gpost_ref, o_ref,
                  kz_ref, vz_ref, pool_tail_ref, conv_tail_ref, shift_ref):
    ts = x_ref.shape[0]
    i = pl.program_id(1)

    @pl.when(i == 0)
    def _():
        kz_ref[:, 0:BLOCK, :] = jnp.zeros((4, BLOCK, LANES), BF16)
        vz_ref[:, 0:BLOCK, :] = jnp.zeros((4, BLOCK, LANES), BF16)
        pool_tail_ref[...] = jnp.zeros_like(pool_tail_ref)
        conv_tail_ref[...] = jnp.zeros_like(conv_tail_ref)

    gpre = gpre_ref[...]
    gpost = gpost_ref[...]
    pscale = pscale_ref[...]
    cw = convw_ref[...]
    q_off = POOL_WIDTH + 3 * CONV_WIDTH
    lo = lax.broadcasted_iota(jnp.int32, (BLOCK, LANES), 1) < HEAD_DIM
    grp = lax.broadcasted_iota(jnp.int32, (BLOCK, POOL_WIDTH), 1) // POOL_GROUP
    win = jnp.where(grp == 0, POOL_WINDOWS[0],
                    jnp.where(grp == 1, POOL_WINDOWS[1],
                              jnp.where(grp == 2, POOL_WINDOWS[2], POOL_WINDOWS[3])))
    row_iota = lax.broadcasted_iota(jnp.int32, (BLOCK, POOL_WIDTH), 0)
    pair_row = lax.broadcasted_iota(jnp.int32, (2 * BLOCK, 1), 0)
    first = jnp.where(i == 0, 1, 0)
    pool_tail = pool_tail_ref[...]
    conv_tail = conv_tail_ref[...]

    for n in range(ts // BLOCK):
        rows = slice(n * BLOCK, (n + 1) * BLOCK)
        xb = x_ref[rows, :]
        hb = _rms_scale(xb, gpre).astype(BF16)

        q = jnp.dot(hb, win_ref[:, q_off:q_off + Q_WIDTH], preferred_element_type=F32)
        qb = (q * (HEAD_DIM ** -0.5)).astype(BF16)
        kv = jnp.dot(hb, win_ref[:, q_off + Q_WIDTH:IN_WIDTH], preferred_element_type=F32)
        new_rows = slice(BLOCK + n * BLOCK, BLOCK + (n + 1) * BLOCK)
        for src, dst in ((kv[:, 0:KV_WIDTH], kz_ref), (kv[:, KV_WIDTH:2 * KV_WIDTH], vz_ref)):
            swapped = pltpu.roll(src, HEAD_DIM, 1)
            zero = jnp.zeros_like(src)
            dst[0, new_rows, :] = jnp.where(lo, src, zero).astype(BF16)
            dst[1, new_rows, :] = jnp.where(lo, zero, swapped).astype(BF16)
            dst[2, new_rows, :] = jnp.where(lo, swapped, zero).astype(BF16)
            dst[3, new_rows, :] = jnp.where(lo, zero, src).astype(BF16)

        pc = jnp.dot(hb, win_ref[:, 0:q_off], preferred_element_type=F32)
        u = pc[:, 0:POOL_WIDTH]
        ext = jnp.concatenate([pool_tail, u], axis=0)
        a2 = ext + pltpu.roll(ext, 1, 0)
        a4 = a2 + pltpu.roll(a2, 2, 0)
        a8 = a4 + pltpu.roll(a4, 4, 0)
        a16 = a8 + pltpu.roll(a8, 8, 0)
        wsum = jnp.where(grp == 0, a2[POOL_TAIL:],
                         jnp.where(grp == 1, a4[POOL_TAIL:],
                                   jnp.where(grp == 2, a8[POOL_TAIL:], a16[POOL_TAIL:])))
        pos = i * ts + n * BLOCK + row_iota
        cnt = jnp.minimum(pos + 1, win).astype(F32)
        d = wsum / cnt - u
        y_pool = jnp.dot(d.astype(BF16), poolw_ref[...], preferred_element_type=F32) * pscale
        pool_tail = u[BLOCK - POOL_TAIL:]

        g_b = pc[:, POOL_WIDTH:POOL_WIDTH + CONV_WIDTH]
        g_c = pc[:, POOL_WIDTH + CONV_WIDTH:POOL_WIDTH + 2 * CONV_WIDTH]
        u_c = pc[:, POOL_WIDTH + 2 * CONV_WIDTH:POOL_WIDTH + 3 * CONV_WIDTH]
        c = g_c * u_c
        y_conv = g_b * _causal_conv3(c, conv_tail, cw, shift_ref, 0)
        conv_tail = c[BLOCK - SUBLANES:]

        bias = bias_ref[first] if n == 0 else bias_ref[0]
        krows = slice(n * BLOCK, n * BLOCK + 2 * BLOCK)
        pieces = [y_pool.astype(BF16), y_conv.astype(BF16)]
        for hk in range(N_KV_HEADS):
            qa = jnp.concatenate([qb[:, (2 * hk) * LANES:(2 * hk + 1) * LANES],
                                  qb[:, (2 * hk + 1) * LANES:(2 * hk + 2) * LANES]], axis=0)
            acc = None
            for half in range(2):
                s = lax.dot_general(qa, kz_ref[2 * hk + half, krows, :],
                                    (((1,), (1,)), ((), ())), preferred_element_type=F32)
                s = s + bias
                head0 = 4 * hk + half
                sink = jnp.where(pair_row < BLOCK, sinks_ref[layer, head0],
                                 sinks_ref[layer, head0 + 2])
                m = jnp.maximum(jnp.max(s, axis=-1, keepdims=True), sink)
                p = jnp.exp(s - m)
                denom = jnp.sum(p, axis=-1, keepdims=True) + jnp.exp(sink - m)
                o = jnp.dot(p.astype(BF16), vz_ref[2 * hk + half, krows, :],
                            preferred_element_type=F32)
                o = o / denom
                acc = o if acc is None else acc + o
            pieces.append(acc[0:BLOCK].astype(BF16))
            pieces.append(acc[BLOCK:2 * BLOCK].astype(BF16))

        mixed = jnp.dot(jnp.concatenate(pieces, axis=1), wo_ref[...],
                        preferred_element_type=F32)
        o_ref[rows, :] = xb + _rms_scale(mixed, gpost)

    pool_tail_ref[...] = pool_tail
    conv_tail_ref[...] = conv_tail
    kz_ref[:, 0:BLOCK, :] = kz_ref[:, ts:ts + BLOCK, :]
    vz_ref[:, 0:BLOCK, :] = vz_ref[:, ts:ts + BLOCK, :]


def _ffn_kernel(x_ref, gpre_ref, wup_ref, cw_ref, cb_ref, wdn_ref, gpost_ref, o_ref,
                tail_ref, act_ref, shift_ref):
    ts = x_ref.shape[0]
    i = pl.program_id(1)

    @pl.when(i == 0)
    def _():
        tail_ref[...] = jnp.zeros_like(tail_ref)

    x = x_ref[...]
    hb = _rms_scale(x, gpre_ref[...]).astype(BF16)

    def conv_cols(cols, slab0):
        up = jnp.dot(hb, wup_ref[:, cols], preferred_element_type=F32)
        out = _causal_conv3(up, tail_ref[:, cols], cw_ref[:, cols], shift_ref, slab0)
        tail_ref[:, cols] = up[ts - SUBLANES:]
        return out + cb_ref[:, cols]

    for c in range(D_FF // FF_CHUNK):
        gate = conv_cols(slice(c * FF_CHUNK, (c + 1) * FF_CHUNK), 0)
        val = conv_cols(slice(D_FF + c * FF_CHUNK, D_FF + (c + 1) * FF_CHUNK),
                        FF_CHUNK // LANES)
        act = gate / (1.0 + jnp.exp(-gate)) * val
        act_ref[:, c * FF_CHUNK:(c + 1) * FF_CHUNK] = act.astype(BF16)

    ff = jnp.dot(act_ref[...], wdn_ref[...], preferred_element_type=F32)
    o_ref[...] = x + _rms_scale(ff, gpost_ref[...])


def _attention_bias():
    qi = jnp.arange(BLOCK, dtype=jnp.int32)[:, None]
    kj = jnp.arange(2 * BLOCK, dtype=jnp.int32)[None, :]
    diff = qi + BLOCK - kj
    band = (diff >= 0) & (diff < WINDOW)
    general = jnp.where(band, 0.0, NEG_INF).astype(F32)
    first = jnp.where(band & (kj >= BLOCK), 0.0, NEG_INF).astype(F32)
    return jnp.stack([jnp.tile(general, (2, 1)), jnp.tile(first, (2, 1))])


def _resident_spec(shape, layer):
    return pl.BlockSpec((None,) + shape, lambda b, i: (layer,) + (0,) * len(shape),
                        pipeline_mode=pl.Buffered(1))


def kernel(x, norm_mix_pre, w_in, pool_w, pool_scale, conv_w, attn_sinks, w_o, norm_mix_post,
           norm_ffn_pre, ffn_w_up, ffn_conv_w, ffn_conv_b, ffn_w_down, norm_ffn_post):
    batch, seq, d = x.shape
    depth = w_in.shape[0]
    assert d == D_MODEL and seq % SEQ_TILE == 0 and SEQ_TILE % BLOCK == 0
    ts = SEQ_TILE
    grid = (batch, seq // ts)

    w_in_b = w_in.astype(BF16)
    w_o_b = w_o.astype(BF16)
    w_up_b = ffn_w_up.astype(BF16)
    w_dn_b = ffn_w_down.astype(BF16)
    eye = jnp.eye(len(POOL_WINDOWS), dtype=F32)
    pool_bd = jnp.einsum('lgcd,gh->lgchd', pool_w, eye).reshape(depth, POOL_WIDTH, POOL_WIDTH)
    pool_bd = pool_bd.astype(BF16)
    bias = _attention_bias()

    def row3(a):
        return a.reshape(depth, 1, a.shape[-1])

    x_spec = pl.BlockSpec((None, ts, d), lambda b, i: (b, i, 0))
    params = pltpu.CompilerParams(dimension_semantics=("arbitrary", "arbitrary"),
                                  vmem_limit_bytes=VMEM_LIMIT_BYTES)

    for layer in range(depth):
        x = pl.pallas_call(
            functools.partial(_mixer_kernel, layer),
            grid=grid,
            in_specs=[
                x_spec,
                _resident_spec((1, d), layer),
                _resident_spec((d, IN_WIDTH), layer),
                _resident_spec((POOL_WIDTH, POOL_WIDTH), layer),
                _resident_spec((1, POOL_WIDTH), layer),
                _resident_spec((CONV_K, CONV_WIDTH), layer),
                pl.BlockSpec(memory_space=pltpu.SMEM),
                pl.BlockSpec((2, 2 * BLOCK, 2 * BLOCK), lambda b, i: (0, 0, 0),
                             pipeline_mode=pl.Buffered(1)),
                _resident_spec((d, d), layer),
                _resident_spec((1, d), layer),
            ],
            out_specs=x_spec,
            out_shape=jax.ShapeDtypeStruct(x.shape, x.dtype),
            scratch_shapes=[
                pltpu.VMEM((4, BLOCK + ts, LANES), BF16),
                pltpu.VMEM((4, BLOCK + ts, LANES), BF16),
                pltpu.VMEM((POOL_TAIL, POOL_WIDTH), F32),
                pltpu.VMEM((SUBLANES, CONV_WIDTH), F32),
                pltpu.VMEM((CONV_WIDTH // LANES, 2 * (SUBLANES + BLOCK), LANES), F32),
            ],
            compiler_params=params,
            name=f"mixer_l{layer}",
        )(x, row3(norm_mix_pre), w_in_b, pool_bd, row3(pool_scale), conv_w, attn_sinks, bias,
          w_o_b, row3(norm_mix_post))

        x = pl.pallas_call(
            _ffn_kernel,
            grid=grid,
            in_specs=[
                x_spec,
                _resident_spec((1, d), layer),
                _resident_spec((d, 2 * D_FF), layer),
                _resident_spec((CONV_K, 2 * D_FF), layer),
                _resident_spec((1, 2 * D_FF), layer),
                _resident_spec((D_FF, d), layer),
                _resident_spec((1, d), layer),
            ],
            out_specs=x_spec,
            out_shape=jax.ShapeDtypeStruct(x.shape, x.dtype),
            scratch_shapes=[
                pltpu.VMEM((SUBLANES, 2 * D_FF), F32),
                pltpu.VMEM((ts, D_FF), BF16),
                pltpu.VMEM((2 * FF_CHUNK // LANES, 2 * (SUBLANES + ts), LANES), F32),
            ],
            compiler_params=params,
            name=f"ffn_l{layer}",
        )(x, row3(norm_ffn_pre), w_up_b, ffn_conv_w, row3(ffn_conv_b), w_dn_b,
          row3(norm_ffn_post))
    return x
```

```python
import functools

import jax
import jax.numpy as jnp
from jax import lax
from jax.experimental import pallas as pl
from jax.experimental.pallas import tpu as pltpu

D_MODEL = 1024
POOL_WINDOWS = (2, 4, 8, 16)
POOL_GROUP = 64
POOL_WIDTH = 256
CONV_WIDTH = 256
CONV_K = 3
HEAD_DIM = 64
N_Q_HEADS = 8
N_KV_HEADS = 2
Q_WIDTH = N_Q_HEADS * HEAD_DIM
KV_WIDTH = N_KV_HEADS * HEAD_DIM
WINDOW = 128
BLOCK = 128
IN_WIDTH = POOL_WIDTH + 3 * CONV_WIDTH + Q_WIDTH + 2 * KV_WIDTH
D_FF = 2816
EPS = 1e-6
NEG_INF = -1e30
LOG2E = 1.4426950408889634

SUBLANES = 8
LANES = 128
POOL_TAIL = 16
SEQ_TILE = 1024
FF_CHUNK = 256
VMEM_LIMIT_BYTES = 56 * 1024 * 1024

F32 = jnp.float32
BF16 = jnp.bfloat16


def _rms_scale(v, gain):
    ms = jnp.mean(v * v, axis=-1, keepdims=True)
    return v * lax.rsqrt(ms + EPS) * gain


def _causal_conv3(cur, tail, cw, buf_ref, slab0):
    ts, width = cur.shape
    delayed = []
    for j in range(width // LANES):
        sl = slice(j * LANES, (j + 1) * LANES)
        buf = buf_ref.at[slab0 + j]
        buf[pl.ds(0, SUBLANES, stride=2), :] = tail[:, sl]
        buf[pl.ds(2 * SUBLANES, ts, stride=2), :] = cur[:, sl]
        d1 = buf[pl.ds(2 * (SUBLANES - 1), ts, stride=2), :]
        d2 = buf[pl.ds(2 * (SUBLANES - 2), ts, stride=2), :]
        delayed.append(cw[0:1, sl] * d2 + cw[1:2, sl] * d1)
    return jnp.concatenate(delayed, axis=1) + cw[2:3] * cur


def _mixer_kernel(layer, x_ref, gpre_ref, win_ref, poolw_ref, pscale_ref, convw_ref, sinks_ref,
                  bias_ref, wo_ref, gpost_ref, o_ref,
                  kz_ref, vz_ref, pool_tail_ref, conv_tail_ref, shift_ref):
    ts = x_ref.shape[0]
    i = pl.program_id(1)

    @pl.when(i == 0)
    def _():
        kz_ref[:, 0:BLOCK, :] = jnp.zeros((4, BLOCK, LANES), BF16)
        vz_ref[:, 0:BLOCK, :] = jnp.zeros((4, BLOCK, LANES), BF16)
        pool_tail_ref[...] = jnp.zeros_like(pool_tail_ref)
        conv_tail_ref[...] = jnp.zeros_like(conv_tail_ref)

    gpre = gpre_ref[...]
    gpost = gpost_ref[...]
    pscale = pscale_ref[...]
    cw = convw_ref[...]
    q_off = POOL_WIDTH + 3 * CONV_WIDTH
    lo = lax.broadcasted_iota(jnp.int32, (BLOCK, LANES), 1) < HEAD_DIM
    grp = lax.broadcasted_iota(jnp.int32, (BLOCK, POOL_WIDTH), 1) // POOL_GROUP
    win = jnp.where(grp == 0, POOL_WINDOWS[0],
                    jnp.where(grp == 1, POOL_WINDOWS[1],
                              jnp.where(grp == 2, POOL_WINDOWS[2], POOL_WINDOWS[3])))
    row_iota = lax.broadcasted_iota(jnp.int32, (BLOCK, POOL_WIDTH), 0)
    pair_row = lax.broadcasted_iota(jnp.int32, (2 * BLOCK, 1), 0)
    first = jnp.where(i == 0, 1, 0)
    pool_tail = pool_tail_ref[...]
    conv_tail = conv_tail_ref[...]

    for n in range(ts // BLOCK):
        rows = slice(n * BLOCK, (n + 1) * BLOCK)
        xb = x_ref[rows, :]
        hb = _rms_scale(xb, gpre).astype(BF16)

        q = jnp.dot(hb, win_ref[:, q_off:q_off + Q_WIDTH], preferred_element_type=F32)
        qb = (q * (HEAD_DIM ** -0.5 * LOG2E)).astype(BF16)
        kv = jnp.dot(hb, win_ref[:, q_off + Q_WIDTH:IN_WIDTH], preferred_element_type=F32)
        new_rows = slice(BLOCK + n * BLOCK, BLOCK + (n + 1) * BLOCK)
        for src, dst in ((kv[:, 0:KV_WIDTH], kz_ref), (kv[:, KV_WIDTH:2 * KV_WIDTH], vz_ref)):
            swapped = pltpu.roll(src, HEAD_DIM, 1)
            zero = jnp.zeros_like(src)
            dst[0, new_rows, :] = jnp.where(lo, src, zero).astype(BF16)
            dst[1, new_rows, :] = jnp.where(lo, zero, swapped).astype(BF16)
            dst[2, new_rows, :] = jnp.where(lo, swapped, zero).astype(BF16)
            dst[3, new_rows, :] = jnp.where(lo, zero, src).astype(BF16)

        pc = jnp.dot(hb, win_ref[:, 0:q_off], preferred_element_type=F32)
        u = pc[:, 0:POOL_WIDTH]
        ext = jnp.concatenate([pool_tail, u], axis=0)
        a2 = ext + pltpu.roll(ext, 1, 0)
        a4 = a2 + pltpu.roll(a2, 2, 0)
        a8 = a4 + pltpu.roll(a4, 4, 0)
        a16 = a8 + pltpu.roll(a8, 8, 0)
        wsum = jnp.where(grp == 0, a2[POOL_TAIL:],
                         jnp.where(grp == 1, a4[POOL_TAIL:],
                                   jnp.where(grp == 2, a8[POOL_TAIL:], a16[POOL_TAIL:])))
        pos = i * ts + n * BLOCK + row_iota
        cnt = jnp.minimum(pos + 1, win).astype(F32)
        d = wsum / cnt - u
        y_pool = jnp.dot(d.astype(BF16), poolw_ref[...], preferred_element_type=F32) * pscale
        pool_tail = u[BLOCK - POOL_TAIL:]

        g_b = pc[:, POOL_WIDTH:POOL_WIDTH + CONV_WIDTH]
        g_c = pc[:, POOL_WIDTH + CONV_WIDTH:POOL_WIDTH + 2 * CONV_WIDTH]
        u_c = pc[:, POOL_WIDTH + 2 * CONV_WIDTH:POOL_WIDTH + 3 * CONV_WIDTH]
        c = g_c * u_c
        y_conv = g_b * _causal_conv3(c, conv_tail, cw, shift_ref, 0)
        conv_tail = c[BLOCK - SUBLANES:]

        bias = bias_ref[first] if n == 0 else bias_ref[0]
        krows = slice(n * BLOCK, n * BLOCK + 2 * BLOCK)
        pieces = [y_pool.astype(BF16), y_conv.astype(BF16)]
        for hk in range(N_KV_HEADS):
            qa = jnp.concatenate([qb[:, (2 * hk) * LANES:(2 * hk + 1) * LANES],
                                  qb[:, (2 * hk + 1) * LANES:(2 * hk + 2) * LANES]], axis=0)
            acc = None
            for half in range(2):
                s = lax.dot_general(qa, kz_ref[2 * hk + half, krows, :],
                                    (((1,), (1,)), ((), ())), preferred_element_type=F32)
                s = s + bias
                head0 = 4 * hk + half
                sink = jnp.where(pair_row < BLOCK, sinks_ref[layer, head0],
                                 sinks_ref[layer, head0 + 2]) * LOG2E
                m = jnp.maximum(jnp.max(s, axis=-1, keepdims=True), sink)
                p = jnp.exp2(s - m)
                denom = jnp.sum(p, axis=-1, keepdims=True) + jnp.exp2(sink - m)
                o = jnp.dot(p.astype(BF16), vz_ref[2 * hk + half, krows, :],
                            preferred_element_type=F32)
                o = o / denom
                acc = o if acc is None else acc + o
            pieces.append(acc[0:BLOCK].astype(BF16))
            pieces.append(acc[BLOCK:2 * BLOCK].astype(BF16))

        mixed = jnp.dot(jnp.concatenate(pieces, axis=1), wo_ref[...],
                        preferred_element_type=F32)
        o_ref[rows, :] = xb + _rms_scale(mixed, gpost)

    pool_tail_ref[...] = pool_tail
    conv_tail_ref[...] = conv_tail
    kz_ref[:, 0:BLOCK, :] = kz_ref[:, ts:ts + BLOCK, :]
    vz_ref[:, 0:BLOCK, :] = vz_ref[:, ts:ts + BLOCK, :]


def _ffn_kernel(x_ref, gpre_ref, wup_ref, cw_ref, cb_ref, wdn_ref, gpost_ref, o_ref,
                tail_ref, act_ref, shift_ref):
    ts = x_ref.shape[0]
    i = pl.program_id(1)

    @pl.when(i == 0)
    def _():
        tail_ref[...] = jnp.zeros_like(tail_ref)

    x = x_ref[...]
    hb = _rms_scale(x, gpre_ref[...]).astype(BF16)

    def conv_cols(cols, slab0):
        up = jnp.dot(hb, wup_ref[:, cols], preferred_element_type=F32)
        out = _causal_conv3(up, tail_ref[:, cols], cw_ref[:, cols], shift_ref, slab0)
        tail_ref[:, cols] = up[ts - SUBLANES:]
        return out + cb_ref[:, cols]

    for c in range(D_FF // FF_CHUNK):
        gate = conv_cols(slice(c * FF_CHUNK, (c + 1) * FF_CHUNK), 0)
        val = conv_cols(slice(D_FF + c * FF_CHUNK, D_FF + (c + 1) * FF_CHUNK),
                        FF_CHUNK // LANES)
        act = gate / (1.0 + jnp.exp2(gate * (-LOG2E))) * val
        act_ref[:, c * FF_CHUNK:(c + 1) * FF_CHUNK] = act.astype(BF16)

    ff = jnp.dot(act_ref[...], wdn_ref[...], preferred_element_type=F32)
    o_ref[...] = x + _rms_scale(ff, gpost_ref[...])


def _attention_bias():
    qi = jnp.arange(BLOCK, dtype=jnp.int32)[:, None]
    kj = jnp.arange(2 * BLOCK, dtype=jnp.int32)[None, :]
    diff = qi + BLOCK - kj
    band = (diff >= 0) & (diff < WINDOW)
    general = jnp.where(band, 0.0, NEG_INF).astype(F32)
    first = jnp.where(band & (kj >= BLOCK), 0.0, NEG_INF).astype(F32)
    return jnp.stack([jnp.tile(general, (2, 1)), jnp.tile(first, (2, 1))])


def _resident_spec(shape, layer):
    return pl.BlockSpec((None,) + shape, lambda b, i: (layer,) + (0,) * len(shape),
                        pipeline_mode=pl.Buffered(1))


def kernel(x, norm_mix_pre, w_in, pool_w, pool_scale, conv_w, attn_sinks, w_o, norm_mix_post,
           norm_ffn_pre, ffn_w_up, ffn_conv_w, ffn_conv_b, ffn_w_down, norm_ffn_post):
    batch, seq, d = x.shape
    depth = w_in.shape[0]
    assert d == D_MODEL and seq % SEQ_TILE == 0 and SEQ_TILE % BLOCK == 0
    ts = SEQ_TILE
    grid = (batch, seq // ts)

    w_in_b = w_in.astype(BF16)
    w_o_b = w_o.astype(BF16)
    w_up_b = ffn_w_up.astype(BF16)
    w_dn_b = ffn_w_down.astype(BF16)
    eye = jnp.eye(len(POOL_WINDOWS), dtype=F32)
    pool_bd = jnp.einsum('lgcd,gh->lgchd', pool_w, eye).reshape(depth, POOL_WIDTH, POOL_WIDTH)
    pool_bd = pool_bd.astype(BF16)
    bias = _attention_bias()

    def row3(a):
        return a.reshape(depth, 1, a.shape[-1])

    x_spec = pl.BlockSpec((None, ts, d), lambda b, i: (b, i, 0))
    params = pltpu.CompilerParams(dimension_semantics=("arbitrary", "arbitrary"),
                                  vmem_limit_bytes=VMEM_LIMIT_BYTES)

    for layer in range(depth):
        x = pl.pallas_call(
            functools.partial(_mixer_kernel, layer),
            grid=grid,
            in_specs=[
                x_spec,
                _resident_spec((1, d), layer),
                _resident_spec((d, IN_WIDTH), layer),
                _resident_spec((POOL_WIDTH, POOL_WIDTH), layer),
                _resident_spec((1, POOL_WIDTH), layer),
                _resident_spec((CONV_K, CONV_WIDTH), layer),
                pl.BlockSpec(memory_space=pltpu.SMEM),
                pl.BlockSpec((2, 2 * BLOCK, 2 * BLOCK), lambda b, i: (0, 0, 0),
                             pipeline_mode=pl.Buffered(1)),
                _resident_spec((d, d), layer),
                _resident_spec((1, d), layer),
            ],
            out_specs=x_spec,
            out_shape=jax.ShapeDtypeStruct(x.shape, x.dtype),
            scratch_shapes=[
                pltpu.VMEM((4, BLOCK + ts, LANES), BF16),
                pltpu.VMEM((4, BLOCK + ts, LANES), BF16),
                pltpu.VMEM((POOL_TAIL, POOL_WIDTH), F32),
                pltpu.VMEM((SUBLANES, CONV_WIDTH), F32),
                pltpu.VMEM((CONV_WIDTH // LANES, 2 * (SUBLANES + BLOCK), LANES), F32),
            ],
            compiler_params=params,
            name=f"mixer_l{layer}",
        )(x, row3(norm_mix_pre), w_in_b, pool_bd, row3(pool_scale), conv_w, attn_sinks, bias,
          w_o_b, row3(norm_mix_post))

        x = pl.pallas_call(
            _ffn_kernel,
            grid=grid,
            in_specs=[
                x_spec,
                _resident_spec((1, d), layer),
                _resident_spec((d, 2 * D_FF), layer),
                _resident_spec((CONV_K, 2 * D_FF), layer),
                _resident_spec((1, 2 * D_FF), layer),
                _resident_spec((D_FF, d), layer),
                _resident_spec((1, d), layer),
            ],
            out_specs=x_spec,
            out_shape=jax.ShapeDtypeStruct(x.shape, x.dtype),
            scratch_shapes=[
                pltpu.VMEM((SUBLANES, 2 * D_FF), F32),
                pltpu.VMEM((ts, D_FF), BF16),
                pltpu.VMEM((2 * FF_CHUNK // LANES, 2 * (SUBLANES + ts), LANES), F32),
            ],
            compiler_params=params,
            name=f"ffn_l{layer}",
        )(x, row3(norm_ffn_pre), w_up_b, ffn_conv_w, row3(ffn_conv_b), w_dn_b,
          row3(norm_ffn_post))
    return x
```

```python
import functools

import jax
import jax.numpy as jnp
from jax import lax
from jax.experimental import pallas as pl
from jax.experimental.pallas import tpu as pltpu

D_MODEL = 1024
POOL_WINDOWS = (2, 4, 8, 16)
POOL_GROUP = 64
POOL_WIDTH = 256
CONV_WIDTH = 256
CONV_K = 3
HEAD_DIM = 64
N_Q_HEADS = 8
N_KV_HEADS = 2
Q_WIDTH = N_Q_HEADS * HEAD_DIM
KV_WIDTH = N_KV_HEADS * HEAD_DIM
WINDOW = 128
BLOCK = 128
IN_WIDTH = POOL_WIDTH + 3 * CONV_WIDTH + Q_WIDTH + 2 * KV_WIDTH
D_FF = 2816
EPS = 1e-6
NEG_INF = -1e30
LOG2E = 1.4426950408889634

SUBLANES = 8
LANES = 128
POOL_TAIL = 16
SEQ_TILE = 1024
FF_CHUNK = 256
FFN_ROW_PARTS = 4
VMEM_LIMIT_BYTES = 56 * 1024 * 1024

F32 = jnp.float32
BF16 = jnp.bfloat16


def _rms_scale(v, gain):
    ms = jnp.mean(v * v, axis=-1, keepdims=True)
    return v * lax.rsqrt(ms + EPS) * gain


def _causal_conv3(cur, tail, cw, buf_ref, slab0):
    ts, width = cur.shape
    delayed = []
    for j in range(width // LANES):
        sl = slice(j * LANES, (j + 1) * LANES)
        buf = buf_ref.at[slab0 + j]
        buf[pl.ds(0, SUBLANES, stride=2), :] = tail[:, sl]
        buf[pl.ds(2 * SUBLANES, ts, stride=2), :] = cur[:, sl]
        d1 = buf[pl.ds(2 * (SUBLANES - 1), ts, stride=2), :]
        d2 = buf[pl.ds(2 * (SUBLANES - 2), ts, stride=2), :]
        delayed.append(cw[0:1, sl] * d2 + cw[1:2, sl] * d1)
    return jnp.concatenate(delayed, axis=1) + cw[2:3] * cur


def _mixer_kernel(layer, x_ref, gpre_ref, win_ref, poolw_ref, pscale_ref, convw_ref, sinks_ref,
                  bias_ref, wo_ref, gpost_ref, o_ref,
                  kz_ref, vz_ref, pool_tail_ref, conv_tail_ref, shift_ref):
    ts = x_ref.shape[0]
    i = pl.program_id(1)

    @pl.when(i == 0)
    def _():
        kz_ref[:, 0:BLOCK, :] = jnp.zeros((4, BLOCK, LANES), BF16)
        vz_ref[:, 0:BLOCK, :] = jnp.zeros((4, BLOCK, LANES), BF16)
        pool_tail_ref[...] = jnp.zeros_like(pool_tail_ref)
        conv_tail_ref[...] = jnp.zeros_like(conv_tail_ref)

    gpre = gpre_ref[...]
    gpost = gpost_ref[...]
    pscale = pscale_ref[...]
    cw = convw_ref[...]
    q_off = POOL_WIDTH + 3 * CONV_WIDTH
    lo = lax.broadcasted_iota(jnp.int32, (BLOCK, LANES), 1) < HEAD_DIM
    grp = lax.broadcasted_iota(jnp.int32, (BLOCK, POOL_WIDTH), 1) // POOL_GROUP
    win = jnp.where(grp == 0, POOL_WINDOWS[0],
                    jnp.where(grp == 1, POOL_WINDOWS[1],
                              jnp.where(grp == 2, POOL_WINDOWS[2], POOL_WINDOWS[3])))
    row_iota = lax.broadcasted_iota(jnp.int32, (BLOCK, POOL_WIDTH), 0)
    pair_row = lax.broadcasted_iota(jnp.int32, (2 * BLOCK, 1), 0)
    first = jnp.where(i == 0, 1, 0)
    n_blocks = ts // BLOCK

    def project(n, pool_tail, conv_tail):
        xb = x_ref[n * BLOCK:(n + 1) * BLOCK, :]
        hb = _rms_scale(xb, gpre).astype(BF16)
        q = jnp.dot(hb, win_ref[:, q_off:q_off + Q_WIDTH], preferred_element_type=F32)
        qb = (q * (HEAD_DIM ** -0.5 * LOG2E)).astype(BF16)
        kv = jnp.dot(hb, win_ref[:, q_off + Q_WIDTH:IN_WIDTH], preferred_element_type=F32)
        new_rows = slice(BLOCK + n * BLOCK, BLOCK + (n + 1) * BLOCK)
        for src, dst in ((kv[:, 0:KV_WIDTH], kz_ref), (kv[:, KV_WIDTH:2 * KV_WIDTH], vz_ref)):
            swapped = pltpu.roll(src, HEAD_DIM, 1)
            zero = jnp.zeros_like(src)
            dst[0, new_rows, :] = jnp.where(lo, src, zero).astype(BF16)
            dst[1, new_rows, :] = jnp.where(lo, zero, swapped).astype(BF16)
            dst[2, new_rows, :] = jnp.where(lo, swapped, zero).astype(BF16)
            dst[3, new_rows, :] = jnp.where(lo, zero, src).astype(BF16)

        pc = jnp.dot(hb, win_ref[:, 0:q_off], preferred_element_type=F32)
        u = pc[:, 0:POOL_WIDTH]
        ext = jnp.concatenate([pool_tail, u], axis=0)
        a2 = ext + pltpu.roll(ext, 1, 0)
        a4 = a2 + pltpu.roll(a2, 2, 0)
        a8 = a4 + pltpu.roll(a4, 4, 0)
        a16 = a8 + pltpu.roll(a8, 8, 0)
        wsum = jnp.where(grp == 0, a2[POOL_TAIL:],
                         jnp.where(grp == 1, a4[POOL_TAIL:],
                                   jnp.where(grp == 2, a8[POOL_TAIL:], a16[POOL_TAIL:])))
        pos = i * ts + n * BLOCK + row_iota
        cnt = jnp.minimum(pos + 1, win).astype(F32)
        d = wsum / cnt - u
        y_pool = jnp.dot(d.astype(BF16), poolw_ref[...], preferred_element_type=F32) * pscale

        g_b = pc[:, POOL_WIDTH:POOL_WIDTH + CONV_WIDTH]
        g_c = pc[:, POOL_WIDTH + CONV_WIDTH:POOL_WIDTH + 2 * CONV_WIDTH]
        u_c = pc[:, POOL_WIDTH + 2 * CONV_WIDTH:POOL_WIDTH + 3 * CONV_WIDTH]
        c = g_c * u_c
        y_conv = g_b * _causal_conv3(c, conv_tail, cw, shift_ref, 0)
        return dict(xb=xb, qb=qb, y_pool=y_pool.astype(BF16), y_conv=y_conv.astype(BF16),
                    pool_tail=u[BLOCK - POOL_TAIL:], conv_tail=c[BLOCK - SUBLANES:])

    def qk_scores(n, qb):
        bias = bias_ref[first] if n == 0 else bias_ref[0]
        krows = slice(n * BLOCK, n * BLOCK + 2 * BLOCK)
        scores = []
        for hk in range(N_KV_HEADS):
            qa = jnp.concatenate([qb[:, (2 * hk) * LANES:(2 * hk + 1) * LANES],
                                  qb[:, (2 * hk + 1) * LANES:(2 * hk + 2) * LANES]], axis=0)
            for half in range(2):
                s = lax.dot_general(qa, kz_ref[2 * hk + half, krows, :],
                                    (((1,), (1,)), ((), ())), preferred_element_type=F32)
                scores.append(s + bias)
        return scores

    def softmax_pv(n, scores):
        krows = slice(n * BLOCK, n * BLOCK + 2 * BLOCK)
        pieces = []
        for hk in range(N_KV_HEADS):
            acc = None
            for half in range(2):
                s = scores[2 * hk + half]
                head0 = 4 * hk + half
                sink = jnp.where(pair_row < BLOCK, sinks_ref[layer, head0],
                                 sinks_ref[layer, head0 + 2]) * LOG2E
                m = jnp.maximum(jnp.max(s, axis=-1, keepdims=True), sink)
                p = jnp.exp2(s - m)
                denom = jnp.sum(p, axis=-1, keepdims=True) + jnp.exp2(sink - m)
                o = jnp.dot(p.astype(BF16), vz_ref[2 * hk + half, krows, :],
                            preferred_element_type=F32)
                o = o / denom
                acc = o if acc is None else acc + o
            pieces.append(acc[0:BLOCK].astype(BF16))
            pieces.append(acc[BLOCK:2 * BLOCK].astype(BF16))
        return pieces

    proj = {0: project(0, pool_tail_ref[...], conv_tail_ref[...])}
    scores = {0: qk_scores(0, proj[0]["qb"])}
    if n_blocks > 1:
        proj[1] = project(1, proj[0]["pool_tail"], proj[0]["conv_tail"])
    for n in range(n_blocks):
        cur = proj.pop(n)
        attn = softmax_pv(n, scores.pop(n))
        if n + 1 < n_blocks:
            scores[n + 1] = qk_scores(n + 1, proj[n + 1]["qb"])
        mix = jnp.concatenate([cur["y_pool"], cur["y_conv"]] + attn, axis=1)
        mixed = jnp.dot(mix, wo_ref[...], preferred_element_type=F32)
        o_ref[n * BLOCK:(n + 1) * BLOCK, :] = cur["xb"] + _rms_scale(mixed, gpost)
        if n + 2 < n_blocks:
            proj[n + 2] = project(n + 2, proj[n + 1]["pool_tail"], proj[n + 1]["conv_tail"])

    pool_tail_ref[...] = cur["pool_tail"]
    conv_tail_ref[...] = cur["conv_tail"]
    kz_ref[:, 0:BLOCK, :] = kz_ref[:, ts:ts + BLOCK, :]
    vz_ref[:, 0:BLOCK, :] = vz_ref[:, ts:ts + BLOCK, :]


def _ffn_kernel(x_ref, gpre_ref, wup_ref, cw_ref, cb_ref, wdn_ref, gpost_ref, o_ref,
                tail_ref, act_ref, shift_ref):
    ts = x_ref.shape[0]
    i = pl.program_id(1)

    @pl.when(i == 0)
    def _():
        tail_ref[...] = jnp.zeros_like(tail_ref)

    pr = ts // FFN_ROW_PARTS
    gpre = gpre_ref[...]
    hb_parts = [_rms_scale(x_ref[p * pr:(p + 1) * pr, :], gpre).astype(BF16)
                for p in range(FFN_ROW_PARTS)]
    hb = jnp.concatenate(hb_parts, axis=0)

    def conv_cols(cols, slab0, by_parts):
        if by_parts:
            up = jnp.concatenate([jnp.dot(h, wup_ref[:, cols], preferred_element_type=F32)
                                  for h in hb_parts], axis=0)
        else:
            up = jnp.dot(hb, wup_ref[:, cols], preferred_element_type=F32)
        out = _causal_conv3(up, tail_ref[:, cols], cw_ref[:, cols], shift_ref, slab0)
        tail_ref[:, cols] = up[ts - SUBLANES:]
        return out + cb_ref[:, cols]

    for c in range(D_FF // FF_CHUNK):
        gate = conv_cols(slice(c * FF_CHUNK, (c + 1) * FF_CHUNK), 0, c == 0)
        val = conv_cols(slice(D_FF + c * FF_CHUNK, D_FF + (c + 1) * FF_CHUNK),
                        FF_CHUNK // LANES, c == 0)
        act = gate / (1.0 + jnp.exp2(gate * (-LOG2E))) * val
        act_ref[:, c * FF_CHUNK:(c + 1) * FF_CHUNK] = act.astype(BF16)

    gpost = gpost_ref[...]
    for p in range(FFN_ROW_PARTS):
        rows = slice(p * pr, (p + 1) * pr)
        ff = jnp.dot(act_ref[rows, :], wdn_ref[...], preferred_element_type=F32)
        o_ref[rows, :] = x_ref[rows, :] + _rms_scale(ff, gpost)


def _attention_bias():
    qi = jnp.arange(BLOCK, dtype=jnp.int32)[:, None]
    kj = jnp.arange(2 * BLOCK, dtype=jnp.int32)[None, :]
    diff = qi + BLOCK - kj
    band = (diff >= 0) & (diff < WINDOW)
    general = jnp.where(band, 0.0, NEG_INF).astype(F32)
    first = jnp.where(band & (kj >= BLOCK), 0.0, NEG_INF).astype(F32)
    return jnp.stack([jnp.tile(general, (2, 1)), jnp.tile(first, (2, 1))])


def _resident_spec(shape, layer):
    return pl.BlockSpec((None,) + shape, lambda b, i: (layer,) + (0,) * len(shape),
                        pipeline_mode=pl.Buffered(1))


def kernel(x, norm_mix_pre, w_in, pool_w, pool_scale, conv_w, attn_sinks, w_o, norm_mix_post,
           norm_ffn_pre, ffn_w_up, ffn_conv_w, ffn_conv_b, ffn_w_down, norm_ffn_post):
    batch, seq, d = x.shape
    depth = w_in.shape[0]
    assert d == D_MODEL and seq % SEQ_TILE == 0 and SEQ_TILE % BLOCK == 0
    ts = SEQ_TILE
    grid = (batch, seq // ts)

    w_in_b = w_in.astype(BF16)
    w_o_b = w_o.astype(BF16)
    w_up_b = ffn_w_up.astype(BF16)
    w_dn_b = ffn_w_down.astype(BF16)
    eye = jnp.eye(len(POOL_WINDOWS), dtype=F32)
    pool_bd = jnp.einsum('lgcd,gh->lgchd', pool_w, eye).reshape(depth, POOL_WIDTH, POOL_WIDTH)
    pool_bd = pool_bd.astype(BF16)
    bias = _attention_bias()

    def row3(a):
        return a.reshape(depth, 1, a.shape[-1])

    x_spec = pl.BlockSpec((None, ts, d), lambda b, i: (b, i, 0))
    params = pltpu.CompilerParams(dimension_semantics=("arbitrary", "arbitrary"),
                                  vmem_limit_bytes=VMEM_LIMIT_BYTES)

    for layer in range(depth):
        x = pl.pallas_call(
            functools.partial(_mixer_kernel, layer),
            grid=grid,
            in_specs=[
                x_spec,
                _resident_spec((1, d), layer),
                _resident_spec((d, IN_WIDTH), layer),
                _resident_spec((POOL_WIDTH, POOL_WIDTH), layer),
                _resident_spec((1, POOL_WIDTH), layer),
                _resident_spec((CONV_K, CONV_WIDTH), layer),
                pl.BlockSpec(memory_space=pltpu.SMEM),
                pl.BlockSpec((2, 2 * BLOCK, 2 * BLOCK), lambda b, i: (0, 0, 0),
                             pipeline_mode=pl.Buffered(1)),
                _resident_spec((d, d), layer),
                _resident_spec((1, d), layer),
            ],
            out_specs=x_spec,
            out_shape=jax.ShapeDtypeStruct(x.shape, x.dtype),
            scratch_shapes=[
                pltpu.VMEM((4, BLOCK + ts, LANES), BF16),
                pltpu.VMEM((4, BLOCK + ts, LANES), BF16),
                pltpu.VMEM((POOL_TAIL, POOL_WIDTH), F32),
                pltpu.VMEM((SUBLANES, CONV_WIDTH), F32),
                pltpu.VMEM((CONV_WIDTH // LANES, 2 * (SUBLANES + BLOCK), LANES), F32),
            ],
            compiler_params=params,
            name=f"mixer_l{layer}",
        )(x, row3(norm_mix_pre), w_in_b, pool_bd, row3(pool_scale), conv_w, attn_sinks, bias,
          w_o_b, row3(norm_mix_post))

        x = pl.pallas_call(
            _ffn_kernel,
            grid=grid,
            in_specs=[
                x_spec,
                _resident_spec((1, d), layer),
                _resident_spec((d, 2 * D_FF), layer),
                _resident_spec((CONV_K, 2 * D_FF), layer),
                _resident_spec((1, 2 * D_FF), layer),
                _resident_spec((D_FF, d), layer),
                _resident_spec((1, d), layer),
            ],
            out_specs=x_spec,
            out_shape=jax.ShapeDtypeStruct(x.shape, x.dtype),
            scratch_shapes=[
                pltpu.VMEM((SUBLANES, 2 * D_FF), F32),
                pltpu.VMEM((ts, D_FF), BF16),
                pltpu.VMEM((2 * FF_CHUNK // LANES, 2 * (SUBLANES + ts), LANES), F32),
            ],
            compiler_params=params,
            name=f"ffn_l{layer}",
        )(x, row3(norm_ffn_pre), w_up_b, ffn_conv_w, row3(ffn_conv_b), w_dn_b,
          row3(norm_ffn_post))
    return x
```
